```python
import math
import jax, jax.numpy as jnp
from jax import lax
import numpy as np

D_MODEL = 1024
BATCH = 16
SEQ = 2048
DEPTH = 2
DEC_BATCH = 32
DEC_SEQ = 16
PAST_LEN = 1024

CHUNK = 64
N_MIXERS = 2
N_RET_LAYERS = (DEPTH + 1) // 2
N_SGU_LAYERS = DEPTH // 2
EPS = 1e-6
RET_HEADS = 4
RET_DK = D_MODEL // RET_HEADS
RET_DV = 2 * RET_DK
RET_QK = RET_HEADS * RET_DK
RET_VDIM = RET_HEADS * RET_DV
RET_IN = 2 * RET_QK + 2 * RET_VDIM
ROPE_BASE = 10000.0
SGU_CHUNK = 128
SGU_GROUPS = 4
SGU_DFF = 6 * D_MODEL
SGU_DH = SGU_DFF // 2
SGU_DG = SGU_DH // SGU_GROUPS
PEER_HEADS = 8
PEER_NKEYS = 128
PEER_NEXP = PEER_NKEYS * PEER_NKEYS
PEER_DK = 256
PEER_HALF = PEER_DK // 2
PEER_TOPK = 16
PEER_BLOCK = 128

kernel_name = 'hybrid_retention_sgu_peer_stream_step'


def rmsnorm(x, g):
    xf = x.astype(jnp.float32)
    y = xf * lax.rsqrt(jnp.mean(xf * xf, axis=-1, keepdims=True) + EPS)
    return (y * g.astype(jnp.float32)).astype(x.dtype)


def rotary(x, pos):
    d = x.shape[-1]
    inv = ROPE_BASE ** (-jnp.arange(0, d, 2, dtype=jnp.float32) / d)
    ang = pos.astype(jnp.float32)[:, None] * inv[None, :]
    cos = jnp.cos(ang)[None, :, None, :]
    sin = jnp.sin(ang)[None, :, None, :]
    xf = x.astype(jnp.float32)
    x1, x2 = xf[..., : d // 2], xf[..., d // 2:]
    return jnp.concatenate([x1 * cos - x2 * sin, x1 * sin + x2 * cos], axis=-1).astype(x.dtype)


def retention_block(s0, q, k, v, log_gamma):
    L = q.shape[2]
    i = jnp.arange(L, dtype=jnp.float32)
    dist = jnp.abs(i[:, None] - i[None, :])
    decay_intra = jnp.exp(log_gamma[:, None, None] * dist).astype(q.dtype)
    scores = jnp.einsum('bhid,bhjd->bhij', q, k) * decay_intra
    o = jnp.einsum('bhij,bhje->bhie', scores, v)
    q_dec = jnp.exp(log_gamma[:, None] * (i + 1.0)).astype(q.dtype)
    o = o + jnp.einsum('bhid,bhde->bhie', q * q_dec[None, :, :, None], s0)
    k_dec = jnp.exp(log_gamma[:, None] * (L - 1.0 - i)).astype(k.dtype)
    chunk_dec = jnp.exp(log_gamma * L).astype(s0.dtype)[None, :, None, None]
    s1 = chunk_dec * s0 + jnp.einsum('bhjd,bhje->bhde', k * k_dec[None, :, :, None], v)
    return o, s1.astype(s0.dtype)


def retention_mixer(h, state0, pos0, w_in, w_o):
    B, S, _ = h.shape
    proj = h @ w_in
    q, k, v, gate = jnp.split(proj, [RET_QK, 2 * RET_QK, 2 * RET_QK + RET_VDIM], axis=-1)
    pos = pos0 + jnp.arange(S, dtype=jnp.int32)
    q = rotary(q.reshape(B, S, RET_HEADS, RET_DK), pos).transpose(0, 2, 1, 3)
    k = (rotary(k.reshape(B, S, RET_HEADS, RET_DK), pos) * (RET_DK ** -0.5)).transpose(0, 2, 1, 3)
    v = v.reshape(B, S, RET_HEADS, RET_DV).transpose(0, 2, 1, 3)
    log_gamma = jnp.log(1.0 - 2.0 ** (-5.0 - jnp.arange(RET_HEADS, dtype=jnp.float32)))
    if S <= CHUNK:
        o, state1 = retention_block(state0, q, k, v, log_gamma)
    else:
        n = S // CHUNK

        def to_chunks(a):
            return a.reshape(B, RET_HEADS, n, CHUNK, a.shape[-1]).transpose(2, 0, 1, 3, 4)

        def step(st, qkv):
            o_c, st = retention_block(st, qkv[0], qkv[1], qkv[2], log_gamma)
            return st, o_c

        state1, oc = lax.scan(step, state0, (to_chunks(q), to_chunks(k), to_chunks(v)))
        o = oc.transpose(1, 2, 0, 3, 4).reshape(B, RET_HEADS, S, RET_DV)
    of = o.astype(jnp.float32)
    mu = jnp.mean(of, axis=-1, keepdims=True)
    var = jnp.mean(jnp.square(of - mu), axis=-1, keepdims=True)
    on = ((of - mu) * lax.rsqrt(var + EPS)).astype(h.dtype)
    on = on.transpose(0, 2, 1, 3).reshape(B, S, RET_VDIM)
    return (jax.nn.silu(gate) * on) @ w_o, state1


def sgu_mixer(h, w_in, g_v, w_s, b_s, w_out):
    B, S, _ = h.shape
    z = jax.nn.gelu(h @ w_in)
    u, v = z[..., :SGU_DH], z[..., SGU_DH:]
    v = rmsnorm(v, g_v)
    C = min(S, SGU_CHUNK)
    idx = jnp.arange(C)
    mask = ((idx[:, None] // CHUNK) >= (idx[None, :] // CHUNK)).astype(h.dtype)
    W = w_s[:, :C, :C] * mask[None]
    vr = v.reshape(B, S // C, C, SGU_GROUPS, SGU_DG)
    mixed = jnp.einsum('gij,bcjgd->bcigd', W, vr) + b_s[:, :C].T[None, None, :, :, None]
    y = u * mixed.reshape(B, S, SGU_DH)
    return y @ w_out, v


def peer(h, w_q, k1, k2, u_tab, v_tab):
    B, S, D = h.shape
    t = h.reshape(-1, D)
    T = t.shape[0]
    nblk = -(-T // PEER_BLOCK)
    t = jnp.pad(t, ((0, nblk * PEER_BLOCK - T), (0, 0))).reshape(nblk, PEER_BLOCK, D)

    def block(tb):
        q = (tb @ w_q).reshape(PEER_BLOCK, PEER_HEADS, PEER_DK)
        s1 = jnp.einsum('thd,nd->thn', q[..., :PEER_HALF], k1).astype(jnp.float32)
        s2 = jnp.einsum('thd,nd->thn', q[..., PEER_HALF:], k2).astype(jnp.float32)
        v1, i1 = lax.top_k(s1, PEER_TOPK)
        v2, i2 = lax.top_k(s2, PEER_TOPK)
        cand = (v1[..., :, None] + v2[..., None, :]).reshape(PEER_BLOCK, PEER_HEADS, PEER_TOPK * PEER_TOPK)
        sc, ci = lax.top_k(cand, PEER_TOPK)
        e = (jnp.take_along_axis(i1, ci // PEER_TOPK, axis=-1) * PEER_NKEYS
             + jnp.take_along_axis(i2, ci % PEER_TOPK, axis=-1))
        g = jax.nn.softmax(sc, axis=-1).astype(tb.dtype)
        ue = u_tab[e]
        ve = v_tab[e]
        a = g * jax.nn.gelu(jnp.einsum('thkd,td->thk', ue, tb))
        return jnp.einsum('thk,thkd->td', a, ve)

    out = lax.map(block, t).reshape(-1, D)[:T]
    return out.reshape(B, S, D)


def setup_inputs(seed: int = 0) -> dict:
    key = jax.random.key(seed)
    ks = jax.random.split(key, 24)
    f32 = jnp.float32
    nrm = lambda k, shape, s: jax.random.normal(k, shape, f32) * s
    gain = lambda k, shape: 1.0 + 0.01 * jax.random.normal(k, shape, f32)
    return {
        'x_prompt': nrm(ks[0], (BATCH, SEQ, D_MODEL), 1.0),
        'x_sample': nrm(ks[1], (DEC_BATCH, DEC_SEQ, D_MODEL), 1.0),
        'state_ret': nrm(ks[2], (N_RET_LAYERS, DEC_BATCH, RET_HEADS, RET_DK, RET_DV), 0.5),
        'ret_norm': gain(ks[3], (N_RET_LAYERS, D_MODEL)),
        'ret_w_in': nrm(ks[4], (N_RET_LAYERS, D_MODEL, RET_IN), D_MODEL ** -0.5),
        'ret_w_o': nrm(ks[5], (N_RET_LAYERS, RET_VDIM, D_MODEL), RET_VDIM ** -0.5),
        'sgu_norm': gain(ks[6], (N_SGU_LAYERS, D_MODEL)),
        'sgu_w_in': nrm(ks[7], (N_SGU_LAYERS, D_MODEL, SGU_DFF), D_MODEL ** -0.5),
        'sgu_v_norm': gain(ks[8], (N_SGU_LAYERS, SGU_DH)),
        'sgu_w_s': nrm(ks[9], (N_SGU_LAYERS, SGU_GROUPS, SGU_CHUNK, SGU_CHUNK), SGU_CHUNK ** -0.5),
        'sgu_b_s': gain(ks[10], (N_SGU_LAYERS, SGU_GROUPS, SGU_CHUNK)),
        'sgu_w_out': nrm(ks[11], (N_SGU_LAYERS, SGU_DH, D_MODEL), SGU_DH ** -0.5),
        'ffn_norm': gain(ks[12], (DEPTH, D_MODEL)),
        'peer_w_q': nrm(ks[13], (DEPTH, D_MODEL, PEER_HEADS * PEER_DK), D_MODEL ** -0.5),
        'peer_k1': nrm(ks[14], (DEPTH, PEER_NKEYS, PEER_HALF), PEER_HALF ** -0.5),
        'peer_k2': nrm(ks[15], (DEPTH, PEER_NKEYS, PEER_HALF), PEER_HALF ** -0.5),
        'peer_u': nrm(ks[16], (DEPTH, PEER_NEXP, D_MODEL), D_MODEL ** -0.5),
        'peer_v': nrm(ks[17], (DEPTH, PEER_NEXP, D_MODEL), 0.25),
        'final_norm': gain(ks[18], (D_MODEL,)),
    }


def reference(x_prompt, x_sample, state_ret, ret_norm, ret_w_in, ret_w_o, sgu_norm, sgu_w_in,
              sgu_v_norm, sgu_w_s, sgu_b_s, sgu_w_out, ffn_norm, peer_w_q, peer_k1, peer_k2,
              peer_u, peer_v, final_norm):
    xp, xs = x_prompt, x_sample
    ret_p, ret_s, sgu_vs = [], [], []
    ri, si = 0, 0
    for layer in range(DEPTH):
        if layer % N_MIXERS == 0:
            hp = rmsnorm(xp, ret_norm[ri])
            hs = rmsnorm(xs, ret_norm[ri])
            zero = jnp.zeros((xp.shape[0], RET_HEADS, RET_DK, RET_DV), xp.dtype)
            op, st_p = retention_mixer(hp, zero, 0, ret_w_in[ri], ret_w_o[ri])
            os_, st_s = retention_mixer(hs, state_ret[ri], PAST_LEN, ret_w_in[ri], ret_w_o[ri])
            ret_p.append(st_p)
            ret_s.append(st_s)
            ri += 1
        else:
            hp = rmsnorm(xp, sgu_norm[si])
            hs = rmsnorm(xs, sgu_norm[si])
            op, _ = sgu_mixer(hp, sgu_w_in[si], sgu_v_norm[si], sgu_w_s[si], sgu_b_s[si], sgu_w_out[si])
            os_, v_s = sgu_mixer(hs, sgu_w_in[si], sgu_v_norm[si], sgu_w_s[si], sgu_b_s[si], sgu_w_out[si])
            sgu_vs.append(v_s)
            si += 1
        xp = xp + op
        xs = xs + os_
        xp = xp + peer(rmsnorm(xp, ffn_norm[layer]), peer_w_q[layer], peer_k1[layer], peer_k2[layer],
                       peer_u[layer], peer_v[layer])
        xs = xs + peer(rmsnorm(xs, ffn_norm[layer]), peer_w_q[layer], peer_k1[layer], peer_k2[layer],
                       peer_u[layer], peer_v[layer])
    y_prompt = rmsnorm(xp, final_norm)
    y_sample = rmsnorm(xs, final_norm)
    new_ret_prompt = jnp.stack(ret_p)
    new_ret_sample = jnp.stack(ret_s)
    new_sgu_v_sample = jnp.stack(sgu_vs)
    return (y_prompt, y_sample, new_ret_prompt, new_ret_sample, new_sgu_v_sample)
```

```python
import functools
import math

import jax
import jax.numpy as jnp
from jax import lax
from jax.experimental import pallas as pl
from jax.experimental.pallas import tpu as pltpu

F32 = jnp.float32
BF16 = jnp.bfloat16

EPS = 1e-6
CHUNK = 64
PAST_LEN = 1024
RET_HEADS = 4
ROPE_BASE = 10000.0
SGU_CHUNK = 128
SGU_GROUPS = 4
PEER_HEADS = 8
PEER_NKEYS = 128
PEER_TOPK = 16

LANES = 128
VMEM_LIMIT = 56 * 1024 * 1024

RET_BLOCK = 256
ROW_BLOCK = 256
PEER_TB = 512
PEER_EB = 512

NT_DIMS = (((1,), (1,)), ((), ()))
TN_DIMS = (((0,), (0,)), ((), ()))


def _params(semantics):
    return pltpu.CompilerParams(dimension_semantics=semantics, vmem_limit_bytes=VMEM_LIMIT)


def _resident(shape):
    zeros = (0,) * len(shape)
    return pl.BlockSpec(shape, lambda *_: zeros, pipeline_mode=pl.Buffered(1))


def _rmsnorm(x, g):
    return x * lax.rsqrt(jnp.mean(x * x, axis=-1, keepdims=True) + EPS) * g


def _norm_matmul_kernel(x_ref, g_ref, w_ref, o_ref, *, n_chunk):
    hb = _rmsnorm(x_ref[...], g_ref[...]).astype(BF16)
    for n0 in range(0, o_ref.shape[1], n_chunk):
        o_ref[:, n0:n0 + n_chunk] = jnp.dot(hb, w_ref[:, n0:n0 + n_chunk], preferred_element_type=F32)


def _norm_matmul(x, g, w):
    t, d = x.shape
    n = w.shape[1]
    return pl.pallas_call(
        functools.partial(_norm_matmul_kernel, n_chunk=1024),
        out_shape=jax.ShapeDtypeStruct((t, n), F32),
        grid=(t // ROW_BLOCK,),
        in_specs=[pl.BlockSpec((ROW_BLOCK, d), lambda i: (i, 0)), _resident((1, d)), _resident((d, n))],
        out_specs=pl.BlockSpec((ROW_BLOCK, n), lambda i: (i, 0)),
        compiler_params=_params(("parallel",)),
        name="norm_matmul",
    )(x, g.reshape(1, d), w)


def _matmul_res_kernel(y_ref, w_ref, x_ref, o_ref):
    o_ref[...] = x_ref[...] + jnp.dot(y_ref[...], w_ref[...], preferred_element_type=F32)


def _matmul_res(y, w, x):
    t, k = y.shape
    d = w.shape[1]
    return pl.pallas_call(
        _matmul_res_kernel,
        out_shape=jax.ShapeDtypeStruct((t, d), F32),
        grid=(t // ROW_BLOCK,),
        in_specs=[pl.BlockSpec((ROW_BLOCK, k), lambda i: (i, 0)), _resident((k, d)),
                  pl.BlockSpec((ROW_BLOCK, d), lambda i: (i, 0))],
        out_specs=pl.BlockSpec((ROW_BLOCK, d), lambda i: (i, 0)),
        compiler_params=_params(("parallel",)),
        name="matmul_res",
    )(y, w, x)


def _ret_kernel(*refs, has_state_in, dk):
    if has_state_in:
        (q_ref, k_ref, v_ref, gate_ref, cos_ref, sin_ref, dec_ref, qd_ref, kd_ref, cd_ref, s0_ref,
         y_ref, s_ref) = refs
    else:
        (q_ref, k_ref, v_ref, gate_ref, cos_ref, sin_ref, dec_ref, qd_ref, kd_ref, cd_ref,
         y_ref, s_ref) = refs
    c = pl.program_id(2)

    @pl.when(c == 0)
    def _():
        if has_state_in:
            s_ref[...] = s0_ref[...]
        else:
            s_ref[...] = jnp.zeros_like(s_ref)

    cos = cos_ref[...]
    sin = sin_ref[...]
    half = dk // 2

    def rot(x):
        x1, x2 = x[:, :half], x[:, half:]
        return jnp.concatenate([x1 * cos - x2 * sin, x1 * sin + x2 * cos], axis=-1)

    q = rot(q_ref[...])
    k = rot(k_ref[...]) * (dk ** -0.5)
    vb = v_ref[...].astype(BF16)
    state = s_ref[...]

    scores = lax.dot_general(q.astype(BF16), k.astype(BF16), NT_DIMS, preferred_element_type=F32)
    scores = scores * dec_ref[...]
    o = jnp.dot(scores.astype(BF16), vb, preferred_element_type=F32)
    o = o + jnp.dot((q * qd_ref[...]).astype(BF16), state.astype(BF16), preferred_element_type=F32)
    kv = lax.dot_general((k * kd_ref[...]).astype(BF16), vb, TN_DIMS, preferred_element_type=F32)
    s_ref[...] = cd_ref[0:1, :] * state + kv

    mu = jnp.mean(o, axis=-1, keepdims=True)
    var = jnp.mean(jnp.square(o - mu), axis=-1, keepdims=True)
    on = (o - mu) * lax.rsqrt(var + EPS)
    gate = gate_ref[...]
    y_ref[...] = (gate * jax.nn.sigmoid(gate) * on).astype(y_ref.dtype)


def _retention_tables(block, pos0, seq, dk):
    inv = ROPE_BASE ** (-jnp.arange(0, dk, 2, dtype=F32) / dk)
    ang = (pos0 + jnp.arange(seq, dtype=jnp.int32)).astype(F32)[:, None] * inv[None, :]
    log_gamma = jnp.log(1.0 - 2.0 ** (-5.0 - jnp.arange(RET_HEADS, dtype=F32)))
    i = jnp.arange(block, dtype=F32)
    ci = jnp.arange(block) // CHUNK
    dist = i[:, None] - i[None, :]
    same = ci[:, None] == ci[None, :]
    seen = ci[:, None] >= ci[None, :]
    dec = jnp.exp(log_gamma[:, None, None] * jnp.where(same, jnp.abs(dist), dist)[None])
    dec = jnp.where(seen[None], dec, 0.0)
    qd = jnp.exp(log_gamma[:, None] * (i + 1.0))
    kd = jnp.exp(log_gamma[:, None] * (block - 1.0 - i))
    cd = jnp.exp(log_gamma * block)
    qd = jnp.broadcast_to(qd[:, :, None], (RET_HEADS, block, dk))
    kd = jnp.broadcast_to(kd[:, :, None], (RET_HEADS, block, dk))
    cd = jnp.broadcast_to(cd[:, None, None], (RET_HEADS, 8, 2 * dk))
    return jnp.cos(ang), jnp.sin(ang), dec, qd, kd, cd


def _retention_core(proj, state0, pos0):
    b, s, n = proj.shape
    dk = n // (6 * RET_HEADS)
    dv = 2 * dk
    blk = min(s, RET_BLOCK)
    nblk = s // blk
    cos, sin, dec, qd, kd, cd = _retention_tables(blk, pos0, s, dk)
    h_ = RET_HEADS
    in_specs = [
        pl.BlockSpec((None, blk, dk), lambda bi, hi, ci: (bi, ci, hi)),
        pl.BlockSpec((None, blk, dk), lambda bi, hi, ci: (bi, ci, h_ + hi)),
        pl.BlockSpec((None, blk, dv), lambda bi, hi, ci: (bi, ci, h_ + hi)),
        pl.BlockSpec((None, blk, dv), lambda bi, hi, ci: (bi, ci, 2 * h_ + hi)),
        pl.BlockSpec((blk, dk // 2), lambda bi, hi, ci: (ci, 0)),
        pl.BlockSpec((blk, dk // 2), lambda bi, hi, ci: (ci, 0)),
        pl.BlockSpec((None, blk, blk), lambda bi, hi, ci: (hi, 0, 0)),
        pl.BlockSpec((None, blk, dk), lambda bi, hi, ci: (hi, 0, 0)),
        pl.BlockSpec((None, blk, dk), lambda bi, hi, ci: (hi, 0, 0)),
        pl.BlockSpec((None, 8, dv), lambda bi, hi, ci: (hi, 0, 0)),
    ]
    args = [proj, proj, proj, proj, cos, sin, dec, qd, kd, cd]
    if state0 is not None:
        in_specs.append(pl.BlockSpec((None, None, dk, dv), lambda bi, hi, ci: (bi, hi, 0, 0)))
        args.append(state0)
    y, state1 = pl.pallas_call(
        functools.partial(_ret_kernel, has_state_in=state0 is not None, dk=dk),
        out_shape=(jax.ShapeDtypeStruct((b, s, h_ * dv), BF16),
                   jax.ShapeDtypeStruct((b, h_, dk, dv), F32)),
        grid=(b, h_, nblk),
        in_specs=in_specs,
        out_specs=(pl.BlockSpec((None, blk, dv), lambda bi, hi, ci: (bi, ci, hi)),
                   pl.BlockSpec((None, None, dk, dv), lambda bi, hi, ci: (bi, hi, 0, 0))),
        compiler_params=_params(("parallel", "parallel", "arbitrary")),
        name="retention",
    )(*args)
    return y, state1


def _retention_layer(x, state0, pos0, g, w_in, w_o):
    b, s, d = x.shape
    xf = x.reshape(b * s, d)
    proj = _norm_matmul(xf, g, w_in)
    y, state1 = _retention_core(proj.reshape(b, s, -1), state0, pos0)
    out = _matmul_res(y.reshape(b * s, -1), w_o, xf)
    return out.reshape(b, s, d), state1


def _sgu_kernel(*refs, emit_v):
    if emit_v:
        x_ref, g_ref, win_ref, gv_ref, mix_ref, bias_ref, wout_ref, o_ref, v_ref = refs
    else:
        x_ref, g_ref, win_ref, gv_ref, mix_ref, bias_ref, wout_ref, o_ref = refs
    x = x_ref[...]
    hb = _rmsnorm(x, g_ref[...]).astype(BF16)
    dh = win_ref.shape[1] // 2
    dg = dh // SGU_GROUPS
    v = jax.nn.gelu(jnp.dot(hb, win_ref[:, dh:], preferred_element_type=F32))
    v = _rmsnorm(v, gv_ref[...])
    if emit_v:
        v_ref[...] = v
    vb = v.astype(BF16)
    acc = x
    for gi in range(SGU_GROUPS):
        cols = slice(gi * dg, (gi + 1) * dg)
        u = jax.nn.gelu(jnp.dot(hb, win_ref[:, cols], preferred_element_type=F32))
        mixed = jnp.dot(mix_ref[gi], vb[:, cols], preferred_element_type=F32) + bias_ref[gi]
        acc = acc + jnp.dot((u * mixed).astype(BF16), wout_ref[cols, :], preferred_element_type=F32)
    o_ref[...] = acc


def _sgu_layer(x, g, w_in, g_v, w_s, b_s, w_out, emit_v):
    b, s, d = x.shape
    t = b * s
    dff = w_in.shape[1]
    dh = dff // 2
    c = min(s, SGU_CHUNK)
    idx = jnp.arange(c)
    mask = ((idx[:, None] // CHUNK) >= (idx[None, :] // CHUNK)).astype(F32)
    w = w_s[:, :c, :c] * mask[None]
    reps = ROW_BLOCK // c
    mix = jnp.einsum("rs,gij->grisj", jnp.eye(reps, dtype=F32), w).reshape(SGU_GROUPS, ROW_BLOCK, ROW_BLOCK)
    bias = jnp.tile(b_s[:, :c], (1, reps))[:, :, None]
    out_shape = [jax.ShapeDtypeStruct((t, d), F32)]
    out_specs = [pl.BlockSpec((ROW_BLOCK, d), lambda i: (i, 0))]
    if emit_v:
        out_shape.append(jax.ShapeDtypeStruct((t, dh), F32))
        out_specs.append(pl.BlockSpec((ROW_BLOCK, dh), lambda i: (i, 0)))
    res = pl.pallas_call(
        functools.partial(_sgu_kernel, emit_v=emit_v),
        out_shape=tuple(out_shape),
        grid=(t // ROW_BLOCK,),
        in_specs=[pl.BlockSpec((ROW_BLOCK, d), lambda i: (i, 0)), _resident((1, d)), _resident((d, dff)),
                  _resident((1, dh)), _resident((SGU_GROUPS, ROW_BLOCK, ROW_BLOCK)),
                  _resident((SGU_GROUPS, ROW_BLOCK, 1)), _resident((dh, d))],
        out_specs=tuple(out_specs),
        compiler_params=_params(("parallel",)),
        name="sgu",
    )(x.reshape(t, d), g.reshape(1, d), w_in, g_v.reshape(1, dh), mix.astype(BF16), bias, w_out)
    out = res[0].reshape(b, s, d)
    return (out, res[1].reshape(b, s, dh)) if emit_v else (out, None)


def _top16(s, vals_ref):
    rank = jnp.full(s.shape, float(PEER_TOPK), F32)
    for r in range(PEER_TOPK):
        m = jnp.max(s, axis=0, keepdims=True)
        hit = s == m
        rank = jnp.where(hit, float(r), rank)
        s = jnp.where(hit, -jnp.inf, s)
        vals_ref[r:r + 1, :] = m
    return rank


def _kth_largest(tiles, k):
    m = None
    for _ in range(k):
        m = functools.reduce(jnp.maximum, tiles)
        m = jnp.max(m, axis=0, keepdims=True)
        tiles = [jnp.where(t == m, -jnp.inf, t) for t in tiles]
    return m


def _peer_rank_kernel(x_ref, g_ref, wq_ref, k1_ref, k2_ref,
                      h_ref, c1_ref, e1_ref, r2_ref, e2_ref,
                      s1_scr, s2_scr, a_scr, b_scr):
    nchunk = x_ref.shape[0] // LANES
    hb = _rmsnorm(x_ref[...], g_ref[...]).astype(BF16)
    h_ref[...] = hb
    qb = jnp.dot(hb, wq_ref[...], preferred_element_type=F32).astype(BF16)
    k1 = k1_ref[...]
    k2 = k2_ref[...]
    half = k1.shape[1]
    for hd in range(PEER_HEADS):
        q1 = qb[:, 2 * hd * half:(2 * hd + 1) * half]
        q2 = qb[:, (2 * hd + 1) * half:(2 * hd + 2) * half]
        s1 = lax.dot_general(k1, q1, NT_DIMS, preferred_element_type=F32)
        s2 = lax.dot_general(k2, q2, NT_DIMS, preferred_element_type=F32)
        for c in range(nchunk):
            s1_scr[hd * nchunk + c] = s1[:, c * LANES:(c + 1) * LANES]
            s2_scr[hd * nchunk + c] = s2[:, c * LANES:(c + 1) * LANES]

    def body(i, carry):
        hd = i // nchunk
        c = i % nchunk
        s1 = s1_scr[i]
        s2 = s2_scr[i]
        rank1 = _top16(s1, a_scr)
        rank2 = _top16(s2, b_scr)
        a_lo, a_hi = a_scr[0:8, :], a_scr[8:16, :]
        b_lo, b_hi = b_scr[0:8, :], b_scr[8:16, :]
        a0, b0 = a_scr[0:1, :], b_scr[0:1, :]
        cands = [a0 + b_lo, a0 + b_hi, a_hi + b0]
        cands += [a_scr[r:r + 1, :] + b_lo for r in range(1, 8)]
        tau = _kth_largest(cands, PEER_TOPK)
        eb_lo, eb_hi = jnp.exp(b_lo - b0), jnp.exp(b_hi - b0)
        z = jnp.zeros_like(a0)
        cnt1 = jnp.zeros_like(s1)
        for r in range(PEER_TOPK):
            ar = a_scr[r:r + 1, :]
            sel_lo = (ar + b_lo) >= tau
            sel_hi = (ar + b_hi) >= tau
            cnt = (jnp.sum(jnp.where(sel_lo, 1.0, 0.0), axis=0, keepdims=True)
                   + jnp.sum(jnp.where(sel_hi, 1.0, 0.0), axis=0, keepdims=True))
            zr = (jnp.sum(jnp.where(sel_lo, eb_lo, 0.0), axis=0, keepdims=True)
                  + jnp.sum(jnp.where(sel_hi, eb_hi, 0.0), axis=0, keepdims=True))
            z = z + jnp.exp(ar - a0) * zr
            cnt1 = jnp.where(rank1 == float(r), cnt, cnt1)
        c1_ref[hd, c] = cnt1
        e1_ref[hd, c] = jnp.exp(s1 - a0) * (1.0 / z)
        r2_ref[hd, c] = rank2
        e2_ref[hd, c] = jnp.exp(s2 - b0)
        return carry

    lax.fori_loop(0, PEER_HEADS * nchunk, body, 0)


def _peer_rank(x, g, w_q, k1, k2):
    t, d = x.shape
    nq = w_q.shape[1]
    nk, half = k1.shape
    tb = PEER_TB
    nchunk = tb // LANES
    tile_shape = jax.ShapeDtypeStruct((PEER_HEADS, t // LANES, nk, LANES), F32)
    tile_spec = pl.BlockSpec((PEER_HEADS, nchunk, nk, LANES), lambda i: (0, i, 0, 0))
    return pl.pallas_call(
        _peer_rank_kernel,
        out_shape=(jax.ShapeDtypeStruct((t, d), BF16), tile_shape, tile_shape, tile_shape, tile_shape),
        grid=(t // tb,),
        in_specs=[pl.BlockSpec((tb, d), lambda i: (i, 0)), _resident((1, d)), _resident((d, nq)),
                  _resident((nk, half)), _resident((nk, half))],
        out_specs=(pl.BlockSpec((tb, d), lambda i: (i, 0)), tile_spec, tile_spec, tile_spec, tile_spec),
        scratch_shapes=[pltpu.VMEM((PEER_HEADS * nchunk, nk, LANES), F32),
                        pltpu.VMEM((PEER_HEADS * nchunk, nk, LANES), F32),
                        pltpu.VMEM((PEER_TOPK, LANES), F32),
                        pltpu.VMEM((PEER_TOPK, LANES), F32)],
        compiler_params=_params(("parallel",)),
        name="peer_rank",
    )(x, g.reshape(1, d), w_q, k1, k2)


def _peer_dense_kernel(*refs, final_norm):
    if final_norm:
        (h_ref, c1_ref, e1_ref, r2_ref, e2_ref, u_ref, vt_ref, x_ref, gf_ref, o_ref, acc_ref, a_ref) = refs
    else:
        (h_ref, c1_ref, e1_ref, r2_ref, e2_ref, u_ref, vt_ref, x_ref, o_ref, acc_ref, a_ref) = refs
    e = pl.program_id(1)
    eb = u_ref.shape[0]
    tb = h_ref.shape[0]
    nk = r2_ref.shape[2]
    rows_per_step = eb // nk

    @pl.when(e == 0)
    def _():
        acc_ref[...] = jnp.zeros_like(acc_ref)

    ut = lax.dot_general(u_ref[...], h_ref[...], NT_DIMS, preferred_element_type=F32)
    for il in range(rows_per_step):
        i = e * rows_per_step + il
        for tc in range(tb // LANES):
            gsum = jnp.zeros((nk, LANES), F32)
            for hd in range(PEER_HEADS):
                c1 = c1_ref[hd, tc, pl.ds(i, 1), :]
                e1 = e1_ref[hd, tc, pl.ds(i, 1), :]
                gsum = gsum + jnp.where(r2_ref[hd, tc] < c1, e2_ref[hd, tc], 0.0) * e1
            u = ut[il * nk:(il + 1) * nk, tc * LANES:(tc + 1) * LANES]
            a_ref[il * nk:(il + 1) * nk, tc * LANES:(tc + 1) * LANES] = (gsum * jax.nn.gelu(u)).astype(BF16)
    acc_ref[...] += jnp.dot(vt_ref[...], a_ref[...], preferred_element_type=F32)

    @pl.when(e == pl.num_programs(1) - 1)
    def _():
        y = x_ref[...] + acc_ref[...].T
        if final_norm:
            y = _rmsnorm(y, gf_ref[...])
        o_ref[...] = y


def _peer_dense(h, c1, e1, r2, e2, u, vt, x, g_final):
    t, d = x.shape
    ne = u.shape[0]
    nk = r2.shape[2]
    tb, eb = PEER_TB, PEER_EB
    nchunk = tb // LANES
    tile_spec = pl.BlockSpec((PEER_HEADS, nchunk, nk, LANES), lambda i, j: (0, i, 0, 0))
    in_specs = [pl.BlockSpec((tb, d), lambda i, j: (i, 0)), tile_spec, tile_spec, tile_spec, tile_spec,
                pl.BlockSpec((eb, d), lambda i, j: (j, 0)), pl.BlockSpec((d, eb), lambda i, j: (0, j)),
                pl.BlockSpec((tb, d), lambda i, j: (i, 0))]
    args = [h, c1, e1, r2, e2, u, vt, x]
    if g_final is not None:
        in_specs.append(pl.BlockSpec((1, d), lambda i, j: (0, 0)))
        args.append(g_final.reshape(1, d))
    return pl.pallas_call(
        functools.partial(_peer_dense_kernel, final_norm=g_final is not None),
        out_shape=jax.ShapeDtypeStruct((t, d), F32),
        grid=(t // tb, ne // eb),
        in_specs=in_specs,
        out_specs=pl.BlockSpec((tb, d), lambda i, j: (i, 0)),
        scratch_shapes=[pltpu.VMEM((d, tb), F32), pltpu.VMEM((eb, tb), BF16)],
        compiler_params=_params(("parallel", "arbitrary")),
        name="peer_dense",
    )(*args)


def _peer_layer(x, g, w_q, k1, k2, u, vt, g_final):
    shape = x.shape
    xf = x.reshape(-1, shape[-1])
    h, c1, e1, r2, e2 = _peer_rank(xf, g, w_q, k1, k2)
    return _peer_dense(h, c1, e1, r2, e2, u, vt, xf, g_final).reshape(shape)


def kernel(x_prompt, x_sample, state_ret, ret_norm, ret_w_in, ret_w_o, sgu_norm, sgu_w_in, sgu_v_norm,
           sgu_w_s, sgu_b_s, sgu_w_out, ffn_norm, peer_w_q, peer_k1, peer_k2, peer_u, peer_v, final_norm):
    depth = ffn_norm.shape[0]
    xp, xs = x_prompt, x_sample
    ret_p, ret_s, sgu_vs = [], [], []
    ri = si = 0
    for layer in range(depth):
        if layer % 2 == 0:
            w_in = ret_w_in[ri].astype(BF16)
            w_o = ret_w_o[ri].astype(BF16)
            xp, st_p = _retention_layer(xp, None, 0, ret_norm[ri], w_in, w_o)
            xs, st_s = _retention_layer(xs, state_ret[ri], PAST_LEN, ret_norm[ri], w_in, w_o)
            ret_p.append(st_p)
            ret_s.append(st_s)
            ri += 1
        else:
            w_in = sgu_w_in[si].astype(BF16)
            w_out = sgu_w_out[si].astype(BF16)
            xp, _ = _sgu_layer(xp, sgu_norm[si], w_in, sgu_v_norm[si], sgu_w_s[si], sgu_b_s[si], w_out, False)
            xs, v_s = _sgu_layer(xs, sgu_norm[si], w_in, sgu_v_norm[si], sgu_w_s[si], sgu_b_s[si], w_out, True)
            sgu_vs.append(v_s)
            si += 1
        w_q = peer_w_q[layer].astype(BF16)
        k1 = peer_k1[layer].astype(BF16)
        k2 = peer_k2[layer].astype(BF16)
        u = peer_u[layer].astype(BF16)
        vt = peer_v[layer].T.astype(BF16)
        g_final = final_norm if layer == depth - 1 else None
        xp = _peer_layer(xp, ffn_norm[layer], w_q, k1, k2, u, vt, g_final)
        xs = _peer_layer(xs, ffn_norm[layer], w_q, k1, k2, u, vt, g_final)
    return (xp, xs, jnp.stack(ret_p), jnp.stack(ret_s), jnp.stack(sgu_vs))
```

```python
import functools
import math

import jax
import jax.numpy as jnp
from jax import lax
from jax.experimental import pallas as pl
from jax.experimental.pallas import tpu as pltpu

F32 = jnp.float32
BF16 = jnp.bfloat16

EPS = 1e-6
CHUNK = 64
PAST_LEN = 1024
RET_HEADS = 4
ROPE_BASE = 10000.0
SGU_CHUNK = 128
SGU_GROUPS = 4
PEER_HEADS = 8
PEER_NKEYS = 128
PEER_TOPK = 16

LANES = 128
VMEM_LIMIT = 56 * 1024 * 1024

RET_BLOCK = 256
ROW_BLOCK = 256
PEER_TB = 512
PEER_EB = 512

NT_DIMS = (((1,), (1,)), ((), ()))
TN_DIMS = (((0,), (0,)), ((), ()))


def _params(semantics, flags=None):
    return pltpu.CompilerParams(dimension_semantics=semantics, vmem_limit_bytes=VMEM_LIMIT, flags=flags)


def _resident(shape):
    zeros = (0,) * len(shape)
    return pl.BlockSpec(shape, lambda *_: zeros, pipeline_mode=pl.Buffered(1))


def _rmsnorm(x, g):
    return x * lax.rsqrt(jnp.mean(x * x, axis=-1, keepdims=True) + EPS) * g


def _norm_matmul_kernel(x_ref, g_ref, w_ref, o_ref, *, n_chunk):
    hb = _rmsnorm(x_ref[...], g_ref[...]).astype(BF16)
    for n0 in range(0, o_ref.shape[1], n_chunk):
        o_ref[:, n0:n0 + n_chunk] = jnp.dot(hb, w_ref[:, n0:n0 + n_chunk], preferred_element_type=F32)


def _norm_matmul(x, g, w):
    t, d = x.shape
    n = w.shape[1]
    return pl.pallas_call(
        functools.partial(_norm_matmul_kernel, n_chunk=1024),
        out_shape=jax.ShapeDtypeStruct((t, n), F32),
        grid=(t // ROW_BLOCK,),
        in_specs=[pl.BlockSpec((ROW_BLOCK, d), lambda i: (i, 0)), _resident((1, d)), _resident((d, n))],
        out_specs=pl.BlockSpec((ROW_BLOCK, n), lambda i: (i, 0)),
        compiler_params=_params(("parallel",)),
        name="norm_matmul",
    )(x, g.reshape(1, d), w)


def _matmul_res_kernel(y_ref, w_ref, x_ref, o_ref):
    o_ref[...] = x_ref[...] + jnp.dot(y_ref[...], w_ref[...], preferred_element_type=F32)


def _matmul_res(y, w, x):
    t, k = y.shape
    d = w.shape[1]
    return pl.pallas_call(
        _matmul_res_kernel,
        out_shape=jax.ShapeDtypeStruct((t, d), F32),
        grid=(t // ROW_BLOCK,),
        in_specs=[pl.BlockSpec((ROW_BLOCK, k), lambda i: (i, 0)), _resident((k, d)),
                  pl.BlockSpec((ROW_BLOCK, d), lambda i: (i, 0))],
        out_specs=pl.BlockSpec((ROW_BLOCK, d), lambda i: (i, 0)),
        compiler_params=_params(("parallel",)),
        name="matmul_res",
    )(y, w, x)


def _ret_kernel(*refs, has_state_in, dk):
    if has_state_in:
        (q_ref, k_ref, v_ref, gate_ref, cos_ref, sin_ref, dec_ref, qd_ref, kd_ref, cd_ref, s0_ref,
         y_ref, s_ref) = refs
    else:
        (q_ref, k_ref, v_ref, gate_ref, cos_ref, sin_ref, dec_ref, qd_ref, kd_ref, cd_ref,
         y_ref, s_ref) = refs
    c = pl.program_id(2)

    @pl.when(c == 0)
    def _():
        if has_state_in:
            s_ref[...] = s0_ref[...]
        else:
            s_ref[...] = jnp.zeros_like(s_ref)

    cos = cos_ref[...]
    sin = sin_ref[...]
    half = dk // 2

    def rot(x):
        x1, x2 = x[:, :half], x[:, half:]
        return jnp.concatenate([x1 * cos - x2 * sin, x1 * sin + x2 * cos], axis=-1)

    q = rot(q_ref[...])
    k = rot(k_ref[...]) * (dk ** -0.5)
    vb = v_ref[...].astype(BF16)
    state = s_ref[...]

    scores = lax.dot_general(q.astype(BF16), k.astype(BF16), NT_DIMS, preferred_element_type=F32)
    scores = scores * dec_ref[...]
    o = jnp.dot(scores.astype(BF16), vb, preferred_element_type=F32)
    o = o + jnp.dot((q * qd_ref[...]).astype(BF16), state.astype(BF16), preferred_element_type=F32)
    kv = lax.dot_general((k * kd_ref[...]).astype(BF16), vb, TN_DIMS, preferred_element_type=F32)
    s_ref[...] = cd_ref[0:1, :] * state + kv

    mu = jnp.mean(o, axis=-1, keepdims=True)
    var = jnp.mean(jnp.square(o - mu), axis=-1, keepdims=True)
    on = (o - mu) * lax.rsqrt(var + EPS)
    gate = gate_ref[...]
    y_ref[...] = (gate * jax.nn.sigmoid(gate) * on).astype(y_ref.dtype)


def _retention_tables(block, pos0, seq, dk):
    inv = ROPE_BASE ** (-jnp.arange(0, dk, 2, dtype=F32) / dk)
    ang = (pos0 + jnp.arange(seq, dtype=jnp.int32)).astype(F32)[:, None] * inv[None, :]
    log_gamma = jnp.log(1.0 - 2.0 ** (-5.0 - jnp.arange(RET_HEADS, dtype=F32)))
    i = jnp.arange(block, dtype=F32)
    ci = jnp.arange(block) // CHUNK
    dist = i[:, None] - i[None, :]
    same = ci[:, None] == ci[None, :]
    seen = ci[:, None] >= ci[None, :]
    dec = jnp.exp(log_gamma[:, None, None] * jnp.where(same, jnp.abs(dist), dist)[None])
    dec = jnp.where(seen[None], dec, 0.0)
    qd = jnp.exp(log_gamma[:, None] * (i + 1.0))
    kd = jnp.exp(log_gamma[:, None] * (block - 1.0 - i))
    cd = jnp.exp(log_gamma * block)
    qd = jnp.broadcast_to(qd[:, :, None], (RET_HEADS, block, dk))
    kd = jnp.broadcast_to(kd[:, :, None], (RET_HEADS, block, dk))
    cd = jnp.broadcast_to(cd[:, None, None], (RET_HEADS, 8, 2 * dk))
    return jnp.cos(ang), jnp.sin(ang), dec, qd, kd, cd


def _retention_core(proj, state0, pos0):
    b, s, n = proj.shape
    dk = n // (6 * RET_HEADS)
    dv = 2 * dk
    blk = min(s, RET_BLOCK)
    nblk = s // blk
    cos, sin, dec, qd, kd, cd = _retention_tables(blk, pos0, s, dk)
    h_ = RET_HEADS
    in_specs = [
        pl.BlockSpec((None, blk, dk), lambda bi, hi, ci: (bi, ci, hi)),
        pl.BlockSpec((None, blk, dk), lambda bi, hi, ci: (bi, ci, h_ + hi)),
        pl.BlockSpec((None, blk, dv), lambda bi, hi, ci: (bi, ci, h_ + hi)),
        pl.BlockSpec((None, blk, dv), lambda bi, hi, ci: (bi, ci, 2 * h_ + hi)),
        pl.BlockSpec((blk, dk // 2), lambda bi, hi, ci: (ci, 0)),
        pl.BlockSpec((blk, dk // 2), lambda bi, hi, ci: (ci, 0)),
        pl.BlockSpec((None, blk, blk), lambda bi, hi, ci: (hi, 0, 0)),
        pl.BlockSpec((None, blk, dk), lambda bi, hi, ci: (hi, 0, 0)),
        pl.BlockSpec((None, blk, dk), lambda bi, hi, ci: (hi, 0, 0)),
        pl.BlockSpec((None, 8, dv), lambda bi, hi, ci: (hi, 0, 0)),
    ]
    args = [proj, proj, proj, proj, cos, sin, dec, qd, kd, cd]
    if state0 is not None:
        in_specs.append(pl.BlockSpec((None, None, dk, dv), lambda bi, hi, ci: (bi, hi, 0, 0)))
        args.append(state0)
    y, state1 = pl.pallas_call(
        functools.partial(_ret_kernel, has_state_in=state0 is not None, dk=dk),
        out_shape=(jax.ShapeDtypeStruct((b, s, h_ * dv), BF16),
                   jax.ShapeDtypeStruct((b, h_, dk, dv), F32)),
        grid=(b, h_, nblk),
        in_specs=in_specs,
        out_specs=(pl.BlockSpec((None, blk, dv), lambda bi, hi, ci: (bi, ci, hi)),
                   pl.BlockSpec((None, None, dk, dv), lambda bi, hi, ci: (bi, hi, 0, 0))),
        compiler_params=_params(("parallel", "parallel", "arbitrary")),
        name="retention",
    )(*args)
    return y, state1


def _retention_layer(x, state0, pos0, g, w_in, w_o):
    b, s, d = x.shape
    xf = x.reshape(b * s, d)
    proj = _norm_matmul(xf, g, w_in)
    y, state1 = _retention_core(proj.reshape(b, s, -1), state0, pos0)
    out = _matmul_res(y.reshape(b * s, -1), w_o, xf)
    return out.reshape(b, s, d), state1


def _sgu_kernel(*refs, emit_v):
    if emit_v:
        x_ref, g_ref, win_ref, gv_ref, mix_ref, bias_ref, wout_ref, o_ref, v_ref = refs
    else:
        x_ref, g_ref, win_ref, gv_ref, mix_ref, bias_ref, wout_ref, o_ref = refs
    x = x_ref[...]
    hb = _rmsnorm(x, g_ref[...]).astype(BF16)
    dh = win_ref.shape[1] // 2
    dg = dh // SGU_GROUPS
    v = jax.nn.gelu(jnp.dot(hb, win_ref[:, dh:], preferred_element_type=F32))
    v = _rmsnorm(v, gv_ref[...])
    if emit_v:
        v_ref[...] = v
    vb = v.astype(BF16)
    acc = x
    for gi in range(SGU_GROUPS):
        cols = slice(gi * dg, (gi + 1) * dg)
        u = jax.nn.gelu(jnp.dot(hb, win_ref[:, cols], preferred_element_type=F32))
        mixed = jnp.dot(mix_ref[gi], vb[:, cols], preferred_element_type=F32) + bias_ref[gi]
        acc = acc + jnp.dot((u * mixed).astype(BF16), wout_ref[cols, :], preferred_element_type=F32)
    o_ref[...] = acc


def _sgu_layer(x, g, w_in, g_v, w_s, b_s, w_out, emit_v):
    b, s, d = x.shape
    t = b * s
    dff = w_in.shape[1]
    dh = dff // 2
    c = min(s, SGU_CHUNK)
    idx = jnp.arange(c)
    mask = ((idx[:, None] // CHUNK) >= (idx[None, :] // CHUNK)).astype(F32)
    w = w_s[:, :c, :c] * mask[None]
    reps = ROW_BLOCK // c
    mix = jnp.einsum("rs,gij->grisj", jnp.eye(reps, dtype=F32), w).reshape(SGU_GROUPS, ROW_BLOCK, ROW_BLOCK)
    bias = jnp.tile(b_s[:, :c], (1, reps))[:, :, None]
    out_shape = [jax.ShapeDtypeStruct((t, d), F32)]
    out_specs = [pl.BlockSpec((ROW_BLOCK, d), lambda i: (i, 0))]
    if emit_v:
        out_shape.append(jax.ShapeDtypeStruct((t, dh), F32))
        out_specs.append(pl.BlockSpec((ROW_BLOCK, dh), lambda i: (i, 0)))
    res = pl.pallas_call(
        functools.partial(_sgu_kernel, emit_v=emit_v),
        out_shape=tuple(out_shape),
        grid=(t // ROW_BLOCK,),
        in_specs=[pl.BlockSpec((ROW_BLOCK, d), lambda i: (i, 0)), _resident((1, d)), _resident((d, dff)),
                  _resident((1, dh)), _resident((SGU_GROUPS, ROW_BLOCK, ROW_BLOCK)),
                  _resident((SGU_GROUPS, ROW_BLOCK, 1)), _resident((dh, d))],
        out_specs=tuple(out_specs),
        compiler_params=_params(("parallel",)),
        name="sgu",
    )(x.reshape(t, d), g.reshape(1, d), w_in, g_v.reshape(1, dh), mix.astype(BF16), bias, w_out)
    out = res[0].reshape(b, s, d)
    return (out, res[1].reshape(b, s, dh)) if emit_v else (out, None)


def _top16(s, vals_ref):
    rank = jnp.full(s.shape, float(PEER_TOPK), F32)
    for r in range(PEER_TOPK):
        m = jnp.max(s, axis=0, keepdims=True)
        hit = s == m
        rank = jnp.where(hit, float(r), rank)
        s = jnp.where(hit, -jnp.inf, s)
        vals_ref[r:r + 1, :] = m
    return rank


def _kth_largest(tiles, k):
    m = None
    for _ in range(k):
        m = functools.reduce(jnp.maximum, tiles)
        m = jnp.max(m, axis=0, keepdims=True)
        tiles = [jnp.where(t == m, -jnp.inf, t) for t in tiles]
    return m


def _dup_bf16(x):
    w = pltpu.bitcast(x.astype(BF16).astype(F32), jnp.uint32)
    return w | (w >> 16)


def _pack_halves(x):
    n = x.shape[0] // 2
    lo = pltpu.bitcast(x[:n].astype(BF16).astype(F32), jnp.uint32)
    hi = pltpu.bitcast(x[n:].astype(BF16).astype(F32), jnp.uint32)
    return (lo >> 16) | hi


def _peer_rank_kernel(x_ref, g_ref, wq_ref, k1_ref, k2_ref,
                      h_ref, c1_ref, e1_ref, r2_ref, e2_ref,
                      s1_scr, s2_scr, a_scr, b_scr):
    nchunk = x_ref.shape[0] // LANES
    hb = _rmsnorm(x_ref[...], g_ref[...]).astype(BF16)
    h_ref[...] = pltpu.bitcast(hb, jnp.uint32)
    qb = jnp.dot(hb, wq_ref[...], preferred_element_type=F32).astype(BF16)
    k1 = k1_ref[...]
    k2 = k2_ref[...]
    half = k1.shape[1]
    for hd in range(PEER_HEADS):
        q1 = qb[:, 2 * hd * half:(2 * hd + 1) * half]
        q2 = qb[:, (2 * hd + 1) * half:(2 * hd + 2) * half]
        s1 = lax.dot_general(k1, q1, NT_DIMS, preferred_element_type=F32)
        s2 = lax.dot_general(k2, q2, NT_DIMS, preferred_element_type=F32)
        for c in range(nchunk):
            s1_scr[hd * nchunk + c] = s1[:, c * LANES:(c + 1) * LANES]
            s2_scr[hd * nchunk + c] = s2[:, c * LANES:(c + 1) * LANES]

    def body(i, carry):
        hd = i // nchunk
        c = i % nchunk
        s1 = s1_scr[i]
        s2 = s2_scr[i]
        rank1 = _top16(s1, a_scr)
        rank2 = _top16(s2, b_scr)
        a_lo, a_hi = a_scr[0:8, :], a_scr[8:16, :]
        b_lo, b_hi = b_scr[0:8, :], b_scr[8:16, :]
        a0, b0 = a_scr[0:1, :], b_scr[0:1, :]
        cands = [a0 + b_lo, a0 + b_hi, a_hi + b0]
        cands += [a_scr[r:r + 1, :] + b_lo for r in range(1, 8)]
        tau = _kth_largest(cands, PEER_TOPK)
        eb_lo, eb_hi = jnp.exp(b_lo - b0), jnp.exp(b_hi - b0)
        z = jnp.zeros_like(a0)
        cnt1 = jnp.zeros_like(s1)
        for r in range(PEER_TOPK):
            ar = a_scr[r:r + 1, :]
            sel_lo = (ar + b_lo) >= tau
            sel_hi = (ar + b_hi) >= tau
            cnt = (jnp.sum(jnp.where(sel_lo, 1.0, 0.0), axis=0, keepdims=True)
                   + jnp.sum(jnp.where(sel_hi, 1.0, 0.0), axis=0, keepdims=True))
            zr = (jnp.sum(jnp.where(sel_lo, eb_lo, 0.0), axis=0, keepdims=True)
                  + jnp.sum(jnp.where(sel_hi, eb_hi, 0.0), axis=0, keepdims=True))
            z = z + jnp.exp(ar - a0) * zr
            cnt1 = jnp.where(rank1 == float(r), cnt, cnt1)
        c1_ref[hd, c] = _dup_bf16(cnt1)
        e1_ref[hd, c] = _dup_bf16(jnp.exp(s1 - a0) * (1.0 / z))
        r2_ref[hd, c] = _pack_halves(rank2)
        e2_ref[hd, c] = _pack_halves(jnp.exp(s2 - b0))
        return carry

    lax.fori_loop(0, PEER_HEADS * nchunk, body, 0)


def _peer_rank(x, g, w_q, k1, k2):
    t, d = x.shape
    nq = w_q.shape[1]
    nk, half = k1.shape
    tb = PEER_TB
    nchunk = tb // LANES
    row_tiles = jax.ShapeDtypeStruct((PEER_HEADS, t // LANES, nk, LANES), jnp.uint32)
    col_tiles = jax.ShapeDtypeStruct((PEER_HEADS, t // LANES, nk // 2, LANES), jnp.uint32)
    row_spec = pl.BlockSpec((PEER_HEADS, nchunk, nk, LANES), lambda i: (0, i, 0, 0))
    col_spec = pl.BlockSpec((PEER_HEADS, nchunk, nk // 2, LANES), lambda i: (0, i, 0, 0))
    return pl.pallas_call(
        _peer_rank_kernel,
        out_shape=(jax.ShapeDtypeStruct((t // 2, d), jnp.uint32), row_tiles, row_tiles, col_tiles, col_tiles),
        grid=(t // tb,),
        in_specs=[pl.BlockSpec((tb, d), lambda i: (i, 0)), _resident((1, d)), _resident((d, nq)),
                  _resident((nk, half)), _resident((nk, half))],
        out_specs=(pl.BlockSpec((tb // 2, d), lambda i: (i, 0)), row_spec, row_spec, col_spec, col_spec),
        scratch_shapes=[pltpu.VMEM((PEER_HEADS * nchunk, nk, LANES), F32),
                        pltpu.VMEM((PEER_HEADS * nchunk, nk, LANES), F32),
                        pltpu.VMEM((PEER_TOPK, LANES), F32),
                        pltpu.VMEM((PEER_TOPK, LANES), F32)],
        compiler_params=_params(("parallel",)),
        name="peer_rank",
    )(x, g.reshape(1, d), w_q, k1, k2)


def _pack_row_pairs(x):
    n, m = x.shape[0] // 2, x.shape[1]
    return lax.bitcast_convert_type(jnp.swapaxes(x.reshape(n, 2, m), 1, 2), jnp.uint32)


def _gelu(x):
    c = math.sqrt(2.0 / math.pi)
    hx = 0.5 * x
    return hx + hx * jnp.tanh(x * (c + (c * 0.044715) * (x * x)))


def _gate_experts(blk, tc, ut_ref, a_ref, c1_ref, e1_ref, r2_ref, e2_ref):
    eb = ut_ref.shape[0]
    nk = c1_ref.shape[2]
    nsub = nk // 16
    nil = eb // nk
    group = 1
    cols = slice(tc * LANES, (tc + 1) * LANES)

    def as_bf16(w):
        return pltpu.bitcast(w, BF16)

    for il0 in range(0, nil, group):
        gate = [[None] * nsub for _ in range(group)]
        for hd in range(PEER_HEADS):
            rows = []
            for g in range(group):
                i = blk * nil + il0 + g
                rows.append((as_bf16(jnp.broadcast_to(c1_ref[hd, tc, pl.ds(i, 1), :], (8, LANES))),
                             as_bf16(jnp.broadcast_to(e1_ref[hd, tc, pl.ds(i, 1), :], (8, LANES)))))
            for s in range(nsub):
                r2 = as_bf16(r2_ref[hd, tc, s * 8:(s + 1) * 8, :])
                e2 = as_bf16(e2_ref[hd, tc, s * 8:(s + 1) * 8, :])
                for g, (c1, e1) in enumerate(rows):
                    term = jnp.where(r2 < c1, e2, jnp.zeros_like(e2)) * e1
                    gate[g][s] = term if gate[g][s] is None else gate[g][s] + term
        for g in range(group):
            words = [pltpu.bitcast(t, jnp.uint32) for t in gate[g]]
            for half in range(2):
                for s in range(0, nsub, 2):
                    r0 = (il0 + g) * nk + half * (nk // 2) + s * 8
                    gv = [pltpu.bitcast(w << 16 if half == 0 else w & jnp.uint32(0xFFFF0000), F32)
                          for w in words[s:s + 2]]
                    a16 = jnp.concatenate(gv, axis=0) * _gelu(ut_ref[r0:r0 + 16, cols])
                    a_ref[r0 // 2:r0 // 2 + 8, cols] = pltpu.bitcast(a16.astype(BF16), jnp.uint32)


def _peer_dense_kernel(*refs, final_norm):
    if final_norm:
        (h_ref, c1_ref, e1_ref, r2_ref, e2_ref, u_ref, vt_ref, x_ref, gf_ref, o_ref,
         acc_ref, ut0_ref, ut1_ref, a0_ref, a1_ref) = refs
    else:
        (h_ref, c1_ref, e1_ref, r2_ref, e2_ref, u_ref, vt_ref, x_ref, o_ref,
         acc_ref, ut0_ref, ut1_ref, a0_ref, a1_ref) = refs
    j = pl.program_id(1)
    last = pl.num_programs(1) - 1
    eb, tb = ut0_ref.shape
    nblk = 2 * last
    tables = (c1_ref, e1_ref, r2_ref, e2_ref)
    mxu_cols = 256
    chunks_per_tile = mxu_cols // LANES

    @pl.when(j == 0)
    def _():
        for ref in (acc_ref, ut0_ref, ut1_ref, a0_ref, a1_ref):
            ref[...] = jnp.zeros_like(ref)

    def stage(half, ut_new, blk_gate, ut_gate, a_gate, a_done):
        u = pltpu.bitcast(u_ref[half * (eb // 2):(half + 1) * (eb // 2), :], BF16)
        vt = pltpu.bitcast(vt_ref[:, half * eb:(half + 1) * eb], BF16)
        for n in range(tb // mxu_cols):
            tok = slice(n * mxu_cols, (n + 1) * mxu_cols)
            h = pltpu.bitcast(h_ref[n * (mxu_cols // 2):(n + 1) * (mxu_cols // 2), :], BF16)
            ut_new[:, tok] = lax.dot_general(u, h, NT_DIMS, preferred_element_type=F32)
            _gate_experts(blk_gate, chunks_per_tile * n, ut_gate, a_gate, *tables)
            a = pltpu.bitcast(a_done[:, tok], BF16)
            acc_ref[:, tok] += jnp.dot(vt, a, preferred_element_type=F32)
            _gate_experts(blk_gate, chunks_per_tile * n + 1, ut_gate, a_gate, *tables)

    stage(0, ut0_ref, jnp.maximum(2 * j - 1, 0), ut1_ref, a1_ref, a0_ref)
    stage(1, ut1_ref, jnp.minimum(2 * j, nblk - 1), ut0_ref, a0_ref, a1_ref)

    @pl.when(j == last)
    def _():
        y = x_ref[...] + acc_ref[...].T
        if final_norm:
            y = _rmsnorm(y, gf_ref[...])
        o_ref[...] = y


def _peer_dense(h, c1, e1, r2, e2, u, vt, x, g_final):
    t, d = x.shape
    ne = 2 * u.shape[0]
    nk = c1.shape[2]
    tb, eb = PEER_TB, PEER_EB
    nchunk = tb // LANES
    npair = ne // (2 * eb)
    row_spec = pl.BlockSpec((PEER_HEADS, nchunk, nk, LANES), lambda i, j: (0, i, 0, 0))
    col_spec = pl.BlockSpec((PEER_HEADS, nchunk, nk // 2, LANES), lambda i, j: (0, i, 0, 0))
    in_specs = [pl.BlockSpec((tb // 2, d), lambda i, j: (i, 0)), row_spec, row_spec, col_spec, col_spec,
                pl.BlockSpec((eb, d), lambda i, j: (jnp.minimum(j, npair - 1), 0)),
                pl.BlockSpec((d // 2, 2 * eb), lambda i, j: (0, jnp.maximum(j - 1, 0))),
                pl.BlockSpec((tb, d), lambda i, j: (i, 0))]
    args = [h, c1, e1, r2, e2, u, vt, x]
    if g_final is not None:
        in_specs.append(pl.BlockSpec((1, d), lambda i, j: (0, 0)))
        args.append(g_final.reshape(1, d))
    return pl.pallas_call(
        functools.partial(_peer_dense_kernel, final_norm=g_final is not None),
        out_shape=jax.ShapeDtypeStruct((t, d), F32),
        grid=(t // tb, npair + 1),
        in_specs=in_specs,
        out_specs=pl.BlockSpec((tb, d), lambda i, j: (i, 0)),
        scratch_shapes=[pltpu.VMEM((d, tb), F32), pltpu.VMEM((eb, tb), F32), pltpu.VMEM((eb, tb), F32),
                        pltpu.VMEM((eb // 2, tb), jnp.uint32), pltpu.VMEM((eb // 2, tb), jnp.uint32)],
        compiler_params=_params(("parallel", "arbitrary")),
        name="peer_dense",
    )(*args)


def _peer_layer(x, g, w_q, k1, k2, u, vt, g_final):
    shape = x.shape
    xf = x.reshape(-1, shape[-1])
    h, c1, e1, r2, e2 = _peer_rank(xf, g, w_q, k1, k2)
    return _peer_dense(h, c1, e1, r2, e2, u, vt, xf, g_final).reshape(shape)


def kernel(x_prompt, x_sample, state_ret, ret_norm, ret_w_in, ret_w_o, sgu_norm, sgu_w_in, sgu_v_norm,
           sgu_w_s, sgu_b_s, sgu_w_out, ffn_norm, peer_w_q, peer_k1, peer_k2, peer_u, peer_v, final_norm):
    depth = ffn_norm.shape[0]
    xp, xs = x_prompt, x_sample
    ret_p, ret_s, sgu_vs = [], [], []
    ri = si = 0
    for layer in range(depth):
        if layer % 2 == 0:
            w_in = ret_w_in[ri].astype(BF16)
            w_o = ret_w_o[ri].astype(BF16)
            xp, st_p = _retention_layer(xp, None, 0, ret_norm[ri], w_in, w_o)
            xs, st_s = _retention_layer(xs, state_ret[ri], PAST_LEN, ret_norm[ri], w_in, w_o)
            ret_p.append(st_p)
            ret_s.append(st_s)
            ri += 1
        else:
            w_in = sgu_w_in[si].astype(BF16)
            w_out = sgu_w_out[si].astype(BF16)
            xp, _ = _sgu_layer(xp, sgu_norm[si], w_in, sgu_v_norm[si], sgu_w_s[si], sgu_b_s[si], w_out, False)
            xs, v_s = _sgu_layer(xs, sgu_norm[si], w_in, sgu_v_norm[si], sgu_w_s[si], sgu_b_s[si], w_out, True)
            sgu_vs.append(v_s)
            si += 1
        w_q = peer_w_q[layer].astype(BF16)
        k1 = peer_k1[layer].astype(BF16)
        k2 = peer_k2[layer].astype(BF16)
        u = _pack_row_pairs(peer_u[layer].astype(BF16))
        vt = _pack_row_pairs(peer_v[layer].T.astype(BF16))
        g_final = final_norm if layer == depth - 1 else None
        xp = _peer_layer(xp, ffn_norm[layer], w_q, k1, k2, u, vt, g_final)
        xs = _peer_layer(xs, ffn_norm[layer], w_q, k1, k2, u, vt, g_final)
    return (xp, xs, jnp.stack(ret_p), jnp.stack(ret_s), jnp.stack(sgu_vs))
```
